```python
import jax, jax.numpy as jnp
from jax import lax
import numpy as np


D_MODEL = 1024
BATCH = 16
SEQ = 256
DEPTH = 2
DEC_BATCH = 2
DEC_SEQ = 2048
PAST_LEN = 512

GRID_W = 64
EXPAND = 2
D_MIX = EXPAND * D_MODEL
D_SSD = D_MIX // 2
D_CONF = D_MIX - D_SSD
SSD_HEAD_DIM = 64
SSD_HEADS = D_SSD // SSD_HEAD_DIM
SSD_GROUPS = 2
HEADS_PER_GROUP = SSD_HEADS // SSD_GROUPS
D_STATE = 128
SSD_CONV_W = 5
SSD_CONV_CH = D_SSD + 2 * SSD_GROUPS * D_STATE
CHUNK = 128
N_DIR = 2
CONF_CONV_W = 31
EPS = 1e-6

I_Z = D_SSD
I_XBC = I_Z + SSD_CONV_CH
I_DT = I_XBC + N_DIR * SSD_HEADS
I_GA = I_DT + D_CONF
I_GB = I_GA + D_CONF
IN_COLS = I_GB + D_CONF

kernel_name = "hybrid_ssd_conformer_diffusion_step"


def rms_norm(x, g):
    xf = x.astype(jnp.float32)
    y = xf * lax.rsqrt(jnp.mean(xf * xf, axis=-1, keepdims=True) + EPS)
    return (y * g.astype(jnp.float32)).astype(x.dtype)


def layer_norm(x, g, b):
    xf = x.astype(jnp.float32)
    mu = jnp.mean(xf, axis=-1, keepdims=True)
    var = jnp.mean(jnp.square(xf - mu), axis=-1, keepdims=True)
    y = (xf - mu) * lax.rsqrt(var + EPS)
    return (y * g.astype(jnp.float32) + b.astype(jnp.float32)).astype(x.dtype)


def depthwise_conv(x, w, b):
    k, ch = w.shape
    out = lax.conv_general_dilated(
        x, w.reshape(k, 1, 1, ch).astype(x.dtype), window_strides=(1, 1),
        padding=((k // 2, k // 2), (0, 0)),
        dimension_numbers=('NHWC', 'HWIO', 'NHWC'), feature_group_count=ch)
    return out + b.astype(x.dtype)


def segsum(a):
    t = a.shape[-1]
    idx = jnp.arange(t)
    xr = jnp.where(idx[:, None] > idx[None, :], a[..., :, None], 0.0)
    s = jnp.cumsum(xr, axis=-2)
    return jnp.where(idx[:, None] >= idx[None, :], s, -jnp.inf)


def ssd_scan(x, dt, a, b_in, c_in, h0):
    bsz, seqlen = x.shape[:2]
    nc = seqlen // CHUNK
    g, r, p, n = SSD_GROUPS, HEADS_PER_GROUP, SSD_HEAD_DIM, D_STATE
    xg = (x * dt[..., None]).reshape(bsz, nc, CHUNK, g, r, p)
    la = (dt * a).reshape(bsz, nc, CHUNK, g, r).transpose(0, 3, 4, 1, 2)
    bc = b_in.reshape(bsz, nc, CHUNK, g, n)
    cc = c_in.reshape(bsz, nc, CHUNK, g, n)
    la_cum = jnp.cumsum(la, axis=-1)
    decay_in = jnp.exp(segsum(la))
    cb = jnp.einsum('bclgn,bcsgn->bgcls', cc, bc)
    y_diag = jnp.einsum('bgcls,bgrcls,bcsgrp->bclgrp', cb, decay_in, xg)
    decay_to_end = jnp.exp(la_cum[..., -1:] - la_cum)
    states = jnp.einsum('bclgn,bgrcl,bclgrp->bcgrpn', bc, decay_to_end, xg)
    h0g = h0.astype(jnp.float32).reshape(bsz, 1, g, r, p, n)
    states = jnp.concatenate([h0g, states], axis=1)
    chunk_tot = jnp.pad(la_cum[..., -1], ((0, 0), (0, 0), (0, 0), (1, 0)))
    decay_chunk = jnp.exp(segsum(chunk_tot))
    states = jnp.einsum('bgrzc,bcgrpn->bzgrpn', decay_chunk, states)
    prev, final = states[:, :-1], states[:, -1]
    y_off = jnp.einsum('bclgn,bcgrpn,bgrcl->bclgrp', cc, prev, jnp.exp(la_cum))
    y = (y_diag + y_off).reshape(bsz, seqlen, SSD_HEADS, p)
    return y, final.reshape(bsz, SSD_HEADS, p, n)


def ssd_branch(z, xbc, dt_raw, conv_w, conv_b, a_log, dt_bias, d_skip, norm_g, h0):
    bsz, seqlen, _ = xbc.shape
    xbc = jax.nn.silu(depthwise_conv(xbc[:, :, None, :], conv_w, conv_b)[:, :, 0, :])
    xs, b_in, c_in = jnp.split(xbc.astype(jnp.float32), [D_SSD, D_SSD + SSD_GROUPS * D_STATE], axis=-1)
    xs = xs.reshape(bsz, seqlen, SSD_HEADS, SSD_HEAD_DIM)
    b_in = b_in.reshape(bsz, seqlen, SSD_GROUPS, D_STATE)
    c_in = c_in.reshape(bsz, seqlen, SSD_GROUPS, D_STATE)
    dt = jax.nn.softplus(dt_raw.astype(jnp.float32).reshape(bsz, seqlen, N_DIR, SSD_HEADS)
                         + dt_bias.astype(jnp.float32))
    a = -jnp.exp(a_log.astype(jnp.float32))
    y_f, h_f = ssd_scan(xs, dt[:, :, 0], a[0], b_in, c_in, h0[:, 0])
    y_b, h_b = ssd_scan(jnp.flip(xs, 1), jnp.flip(dt[:, :, 1], 1), a[1],
                        jnp.flip(b_in, 1), jnp.flip(c_in, 1), h0[:, 1])
    y = y_f + jnp.flip(y_b, 1) + d_skip.astype(jnp.float32)[:, None] * xs
    y = y.reshape(bsz, seqlen, D_SSD) * jax.nn.silu(z.astype(jnp.float32))
    y = rms_norm(y, norm_g)
    return y.astype(z.dtype), jnp.stack([h_f, h_b], axis=1)


def conformer_branch(ga, gb, conv_w, conv_b, ln_g, ln_b):
    h = ga * jax.nn.sigmoid(gb)
    h = depthwise_conv(h, conv_w, conv_b)
    h = layer_norm(h, ln_g, ln_b)
    return jax.nn.silu(h)


def trunk_layer(x, mod, grid, h0, g_pre, g_post, w_in, ssd_conv_w, ssd_conv_b, a_log, dt_bias,
                d_skip, ssd_norm_g, conf_conv_w, conf_conv_b, conf_ln_g, conf_ln_b, w_out):
    bsz, seqlen, _ = x.shape
    shift, scale, gate = jnp.split(mod[:, None, :].astype(x.dtype), 3, axis=-1)
    h = rms_norm(x, g_pre) * (1 + scale) + shift
    u = h @ w_in
    z, xbc, dt_raw, ga, gb, gsil = jnp.split(u, [I_Z, I_XBC, I_DT, I_GA, I_GB], axis=-1)
    y_ssd, h_fin = ssd_branch(z, xbc, dt_raw, ssd_conv_w, ssd_conv_b, a_log, dt_bias, d_skip,
                              ssd_norm_g, h0)
    rows, cols = grid
    y_conf = conformer_branch(ga.reshape(bsz, rows, cols, D_CONF), gb.reshape(bsz, rows, cols, D_CONF),
                              conf_conv_w, conf_conv_b, conf_ln_g, conf_ln_b)
    y_conf = y_conf.reshape(bsz, seqlen, D_CONF) * jax.nn.silu(gsil)
    out = jnp.concatenate([y_ssd.astype(x.dtype), y_conf.astype(x.dtype)], axis=-1) @ w_out
    return x + gate * rms_norm(out, g_post), h_fin


def setup_inputs(seed: int = 0) -> dict:
    key = jax.random.key(seed)
    ks = jax.random.split(key, 24)
    f32 = jnp.float32
    nrm = lambda k, shape, s: jax.random.normal(k, shape, f32) * s
    dt0 = jnp.exp(jax.random.uniform(ks[12], (DEPTH, N_DIR, SSD_HEADS), f32,
                                     np.log(1e-3).astype(np.float32), np.log(1e-1).astype(np.float32)))
    return {
        "x_prompt": nrm(ks[0], (BATCH, SEQ, D_MODEL), 1.0),
        "x_sample": nrm(ks[1], (DEC_BATCH, DEC_SEQ, D_MODEL), 1.0),
        "state_ssd": nrm(ks[2], (DEC_BATCH, DEPTH, N_DIR, SSD_HEADS, SSD_HEAD_DIM, D_STATE), 0.1),
        "c": nrm(ks[3], (DEC_BATCH, D_MODEL), 1.0),
        "c_ctx": nrm(ks[4], (D_MODEL,), 1.0),
        "w_mod": nrm(ks[5], (DEPTH, D_MODEL, 3 * D_MODEL), 0.2 * D_MODEL ** -0.5),
        "b_mod": nrm(ks[6], (DEPTH, 3 * D_MODEL), 0.02),
        "g_pre": 1.0 + nrm(ks[7], (DEPTH, D_MODEL), 0.1),
        "g_post": 1.0 + nrm(ks[8], (DEPTH, D_MODEL), 0.1),
        "w_in": nrm(ks[9], (DEPTH, D_MODEL, IN_COLS), D_MODEL ** -0.5),
        "ssd_conv_w": nrm(ks[10], (DEPTH, SSD_CONV_W, SSD_CONV_CH), SSD_CONV_W ** -0.5),
        "ssd_conv_b": nrm(ks[11], (DEPTH, SSD_CONV_CH), 0.01),
        "ssd_a_log": jnp.log(jax.random.uniform(ks[13], (DEPTH, N_DIR, SSD_HEADS), f32, 1.0, 16.0)),
        "ssd_dt_bias": dt0 + jnp.log(-jnp.expm1(-dt0)),
        "ssd_d": 1.0 + nrm(ks[14], (DEPTH, SSD_HEADS), 0.1),
        "ssd_norm_g": 1.0 + nrm(ks[15], (DEPTH, D_SSD), 0.1),
        "conf_conv_w": nrm(ks[16], (DEPTH, CONF_CONV_W, D_CONF), CONF_CONV_W ** -0.5),
        "conf_conv_b": nrm(ks[17], (DEPTH, D_CONF), 0.01),
        "conf_ln_g": 1.0 + nrm(ks[18], (DEPTH, D_CONF), 0.1),
        "conf_ln_b": nrm(ks[19], (DEPTH, D_CONF), 0.01),
        "w_out": nrm(ks[20], (DEPTH, D_MIX, D_MODEL), D_MIX ** -0.5),
    }


def reference(x_prompt, x_sample, state_ssd, c, c_ctx, w_mod, b_mod, g_pre, g_post, w_in,
              ssd_conv_w, ssd_conv_b, ssd_a_log, ssd_dt_bias, ssd_d, ssd_norm_g,
              conf_conv_w, conf_conv_b, conf_ln_g, conf_ln_b, w_out):
    ctx_b, ctx_len = x_prompt.shape[0], x_prompt.shape[1]
    rows = x_sample.shape[1] // GRID_W
    ctx_grid = (ctx_len, 1)
    lat_grid = (rows, GRID_W)
    silu_ctx = jax.nn.silu(c_ctx)[None, :]
    silu_c = jax.nn.silu(c)
    h0_ctx = jnp.zeros((ctx_b, N_DIR, SSD_HEADS, SSD_HEAD_DIM, D_STATE), jnp.float32)
    xp, xs = x_prompt, x_sample
    ctx_states = []
    for l in range(DEPTH):
        lp = (g_pre[l], g_post[l], w_in[l], ssd_conv_w[l], ssd_conv_b[l], ssd_a_log[l], ssd_dt_bias[l],
              ssd_d[l], ssd_norm_g[l], conf_conv_w[l], conf_conv_b[l], conf_ln_g[l], conf_ln_b[l], w_out[l])
        mod_ctx = silu_ctx @ w_mod[l] + b_mod[l]
        xp, st = trunk_layer(xp, mod_ctx, ctx_grid, h0_ctx, *lp)
        ctx_states.append(st.astype(x_prompt.dtype))
        mod_lat = silu_c @ w_mod[l] + b_mod[l]
        xs, _ = trunk_layer(xs, mod_lat, lat_grid, state_ssd[:, l], *lp)
    new_state_ssd = jnp.stack(ctx_states, axis=1)
    return (xp, xs, new_state_ssd)
```

```python
import functools

import jax
import jax.numpy as jnp
from jax import lax
from jax.experimental import pallas as pl
from jax.experimental.pallas import tpu as pltpu

F32 = jnp.float32
BF16 = jnp.bfloat16

D_MODEL = 1024
DEPTH = 2
GRID_W = 64
D_SSD = 1024
D_CONF = 1024
HEAD_DIM = 64
N_HEADS = 16
N_GROUPS = 2
HEADS_PER_GROUP = N_HEADS // N_GROUPS
GROUP_W = HEADS_PER_GROUP * HEAD_DIM
D_STATE = 128
SSD_CONV_W = 5
CHUNK = 128
N_DIR = 2
CONF_CONV_W = 31
EPS = 1e-6

I_Z = D_SSD
I_X = I_Z + D_SSD
I_XBC = I_X + 2 * N_GROUPS * D_STATE
I_DT = I_XBC + N_DIR * N_HEADS
I_GA = I_DT + D_CONF
I_GB = I_GA + D_CONF
IN_COLS = I_GB + D_CONF

LANES = 128
SUBLANES = 8
DT_W = LANES
C_Z = 0
C_X = C_Z + D_SSD
C_BC = C_X + D_SSD
C_GA = C_BC + 2 * N_GROUPS * D_STATE
C_GB = C_GA + D_CONF
C_GS = C_GB + D_CONF
C_DT = C_GS + D_CONF
C_END = C_DT + N_GROUPS * DT_W

VMEM_LIMIT = 56 * 1024 * 1024
ROW_TILE = 256
CONV_PAD = 8
CONF_HALF = CONF_CONV_W // 2
NT_DIMS = (((1,), (1,)), ((), ()))


def _sigmoid(x):
    return 1.0 / (1.0 + jnp.exp(-x))


def _silu(x):
    return x * _sigmoid(x)


def _softplus(x):
    return jnp.maximum(x, 0.0) + jnp.log1p(jnp.exp(-jnp.abs(x)))


def _split3(x):
    hi = x.astype(BF16)
    r1 = x - hi.astype(F32)
    mid = r1.astype(BF16)
    lo = (r1 - mid.astype(F32)).astype(BF16)
    return hi, mid, lo


def _mod_kernel(c_ref, w_ref, b_ref, o_ref):
    s = _silu(c_ref[...])
    o_ref[...] = jnp.dot(s, w_ref[...], preferred_element_type=F32,
                         precision=lax.Precision.HIGHEST) + b_ref[...]


def _modulation(cvec, w_mod, b_mod):
    n_col = 3 * D_MODEL // D_MODEL
    return pl.pallas_call(
        _mod_kernel,
        grid=(DEPTH, n_col),
        in_specs=[
            pl.BlockSpec((SUBLANES, D_MODEL), lambda l, j: (0, 0)),
            pl.BlockSpec((None, D_MODEL, D_MODEL), lambda l, j: (l, 0, j)),
            pl.BlockSpec((None, 1, D_MODEL), lambda l, j: (l, 0, j)),
        ],
        out_specs=pl.BlockSpec((None, SUBLANES, D_MODEL), lambda l, j: (l, 0, j)),
        out_shape=jax.ShapeDtypeStruct((DEPTH, SUBLANES, 3 * D_MODEL), F32),
        compiler_params=pltpu.CompilerParams(
            dimension_semantics=("arbitrary", "arbitrary"), vmem_limit_bytes=VMEM_LIMIT),
        name="modulation",
    )(cvec, w_mod, b_mod.reshape(DEPTH, 1, 3 * D_MODEL))


def _inproj_kernel(x_ref, shift_ref, scale_ref, g_ref, w_ref,
                   z_ref, xs_ref, bc_ref, ga_ref, gb_ref, gs_ref, dt_ref):
    x = x_ref[...]
    ms = jnp.mean(x * x, axis=-1, keepdims=True)
    h = (x * lax.rsqrt(ms + EPS) * g_ref[...]) * (1.0 + scale_ref[...]) + shift_ref[...]
    hb = h.astype(BF16)

    def proj(lo, hi):
        return jnp.dot(hb, w_ref[:, lo:hi], preferred_element_type=F32)

    z_ref[...] = proj(C_Z, C_X)
    xs_ref[...] = proj(C_X, C_BC)
    bc_ref[...] = proj(C_BC, C_GA)
    ga_ref[...] = proj(C_GA, C_GB)
    gb_ref[...] = proj(C_GB, C_GS)
    gs_ref[...] = proj(C_GS, C_DT)
    dt_ref[...] = proj(C_DT, C_END)


def _in_projection(x2, shift, scale, g_pre, w_in_k, rows_per_mod):
    t = x2.shape[0]
    tiles_per_mod = rows_per_mod // ROW_TILE
    row = lambda i: (i, 0)
    mod = lambda i: (i // tiles_per_mod, 0, 0)
    const = lambda i: (0, 0)
    widths = (D_SSD, D_SSD, 2 * N_GROUPS * D_STATE, D_CONF, D_CONF, D_CONF, N_GROUPS * DT_W)
    return pl.pallas_call(
        _inproj_kernel,
        grid=(t // ROW_TILE,),
        in_specs=[
            pl.BlockSpec((ROW_TILE, D_MODEL), row),
            pl.BlockSpec((None, 1, D_MODEL), mod),
            pl.BlockSpec((None, 1, D_MODEL), mod),
            pl.BlockSpec((1, D_MODEL), const),
            pl.BlockSpec((D_MODEL, C_END), const),
        ],
        out_specs=[pl.BlockSpec((ROW_TILE, w), row) for w in widths],
        out_shape=[jax.ShapeDtypeStruct((t, w), F32) for w in widths],
        compiler_params=pltpu.CompilerParams(
            dimension_semantics=("arbitrary",), vmem_limit_bytes=VMEM_LIMIT),
        name="in_projection",
    )(x2, shift, scale, g_pre, w_in_k)


def _ssd_kernel(xs_ref, b_ref, c_ref, dt_ref, h0_ref, cwx_ref, cwb_ref, cwc_ref,
                cbx_ref, cbb_ref, cbc_ref, alog_ref, dtb_ref, dsk_ref,
                y_ref, hfin_ref,
                xpad, bpad, cpad, ccv, cums, yt, stf, stb, s_f, s_b,
                *, seq_len, zero_h0):
    nc = seq_len // CHUNK
    hw = HEAD_DIM

    for pad, src in ((xpad, xs_ref), (bpad, b_ref), (cpad, c_ref)):
        zeros = jnp.zeros((CONV_PAD, pad.shape[1]), F32)
        pad[0:CONV_PAD, :] = zeros
        pad[CONV_PAD + seq_len:2 * CONV_PAD + seq_len, :] = zeros
        pad[CONV_PAD:CONV_PAD + seq_len, :] = src[...]

    row_i = lax.broadcasted_iota(jnp.int32, (CHUNK, CHUNK), 0)
    col_i = lax.broadcasted_iota(jnp.int32, (CHUNK, CHUNK), 1)
    lower = row_i >= col_i
    upper = row_i <= col_i
    tri = jnp.concatenate([jnp.where(lower, 1.0, 0.0), jnp.where(upper, 1.0, 0.0)],
                          axis=1).astype(BF16)
    lane = lax.broadcasted_iota(jnp.int32, (CHUNK, LANES), 1)
    fwd_lanes = lane < HEADS_PER_GROUP
    bwd_lanes = (lane >= HEADS_PER_GROUP) & (lane < 2 * HEADS_PER_GROUP)
    a_neg = -jnp.exp(alog_ref[...])
    dt_bias = dtb_ref[...]

    def conv_silu(pad, w_ref, bias_ref, o):
        win = pad[pl.ds(o, CHUNK + 2 * CONV_PAD), :]
        acc = jnp.broadcast_to(bias_ref[...], (CHUNK, pad.shape[1]))
        for k in range(SSD_CONV_W):
            lo = CONV_PAD - SSD_CONV_W // 2 + k
            acc = acc + w_ref[k:k + 1, :] * win[lo:lo + CHUNK, :]
        return _silu(acc)

    def chunk_local(c, carry):
        o = pl.multiple_of(c * CHUNK, CHUNK)
        xs_c = conv_silu(xpad, cwx_ref, cbx_ref, o)
        b_c = conv_silu(bpad, cwb_ref, cbb_ref, o)
        c_c = conv_silu(cpad, cwc_ref, cbc_ref, o)
        ccv[pl.ds(o, CHUNK), :] = c_c
        b_bf = b_c.astype(BF16)
        cb = lax.dot_general(c_c.astype(BF16), b_bf, NT_DIMS, preferred_element_type=F32)
        xs_t = xs_c.T

        dt_c = _softplus(dt_ref[pl.ds(o, CHUNK), :] + dt_bias)
        la = dt_c * a_neg
        stacked = jnp.concatenate(
            [jnp.where(fwd_lanes, la, 0.0), jnp.where(bwd_lanes, la, 0.0)], axis=0)
        cum = None
        for part in _split3(stacked):
            term = jnp.dot(tri, part, preferred_element_type=F32)
            cum = term if cum is None else cum + term
        cum_t = cum.T
        dt_t = dt_c.T
        cums[c] = cum_t[0:2 * HEADS_PER_GROUP, :]

        y_rows = []
        for j in range(HEADS_PER_GROUP):
            x_h = xs_t[j * hw:(j + 1) * hw, :]
            y_rows.append(dsk_ref[j * hw:(j + 1) * hw, :] * x_h)
        for d, (mask, st_ref) in enumerate(((lower, stf), (upper, stb))):
            xgd_rows = []
            for j in range(HEADS_PER_GROUP):
                q = d * HEADS_PER_GROUP + j
                crow = cum_t[q:q + 1, :]
                diff = cum[:, q:q + 1] - crow
                decay = jnp.exp(jnp.where(mask, diff, -jnp.inf))
                m_h = (cb * decay).astype(BF16)
                xg_t = xs_t[j * hw:(j + 1) * hw, :] * dt_t[q:q + 1, :]
                y_rows[j] = y_rows[j] + lax.dot_general(
                    xg_t.astype(BF16), m_h, NT_DIMS, preferred_element_type=F32)
                tot = crow[:, CHUNK - 1:CHUNK] if d == 0 else crow[:, 0:1]
                xgd_rows.append(xg_t * jnp.exp(tot - crow))
            xgd = jnp.concatenate(xgd_rows, axis=0).astype(BF16)
            st_ref[c] = jnp.dot(xgd, b_bf, preferred_element_type=F32)
        yt[c] = jnp.concatenate(y_rows, axis=0)
        return carry

    lax.fori_loop(0, nc, chunk_local, 0)

    for d, s_ref in enumerate((s_f, s_b)):
        if zero_h0:
            s_ref[...] = jnp.zeros(s_ref.shape, F32)
        else:
            s_ref[...] = h0_ref[d].reshape(GROUP_W, D_STATE)

    def head_rows(ct, base, col):
        parts = []
        for j in range(HEADS_PER_GROUP):
            r = ct[base + j:base + j + 1, :]
            if col is not None:
                r = r[:, col:col + 1]
            parts.append(jnp.broadcast_to(jnp.exp(r), (hw, CHUNK)))
        return jnp.concatenate(parts, axis=0)

    def recur(i, carry):
        for d, (s_ref, st_ref) in enumerate(((s_f, stf), (s_b, stb))):
            c = i if d == 0 else nc - 1 - i
            o = pl.multiple_of(c * CHUNK, CHUNK)
            s = s_ref[...]
            c_bf = ccv[pl.ds(o, CHUNK), :].astype(BF16)
            y_off = lax.dot_general(s.astype(BF16), c_bf, NT_DIMS, preferred_element_type=F32)
            ct = cums[c]
            base = d * HEADS_PER_GROUP
            yt[c] = yt[c] + y_off * head_rows(ct, base, None)
            s_ref[...] = s * head_rows(ct, base, CHUNK - 1 if d == 0 else 0) + st_ref[c]
        return carry

    lax.fori_loop(0, nc, recur, 0)

    hfin_ref[0] = s_f[...].reshape(HEADS_PER_GROUP, HEAD_DIM, D_STATE)
    hfin_ref[1] = s_b[...].reshape(HEADS_PER_GROUP, HEAD_DIM, D_STATE)

    def write_back(c, carry):
        o = pl.multiple_of(c * CHUNK, CHUNK)
        y_ref[pl.ds(o, CHUNK), :] = yt[c].T
        return carry

    lax.fori_loop(0, nc, write_back, 0)


def _ssd_branch(xs, bc, dt, h0, h0_layer, p, batch, seq_len):
    nc = seq_len // CHUNK
    zero_h0 = h0 is None
    seq_blk = lambda w, col: pl.BlockSpec((None, seq_len, w), lambda b, g: (b, 0, col(g)))
    if zero_h0:
        h0 = jnp.zeros((1, 1, N_DIR, HEADS_PER_GROUP, HEAD_DIM, D_STATE), F32)
        h0_spec = pl.BlockSpec((None, None, N_DIR, HEADS_PER_GROUP, HEAD_DIM, D_STATE),
                               lambda b, g: (0, 0, 0, 0, 0, 0))
    else:
        h0_spec = pl.BlockSpec((None, None, N_DIR, HEADS_PER_GROUP, HEAD_DIM, D_STATE),
                               lambda b, g: (b, h0_layer, 0, g, 0, 0))
    n_xblk = D_SSD // D_STATE
    in_specs = [
        seq_blk(GROUP_W, lambda g: g),
        seq_blk(D_STATE, lambda g: g),
        seq_blk(D_STATE, lambda g: N_GROUPS + g),
        seq_blk(DT_W, lambda g: g),
        h0_spec,
        pl.BlockSpec((SSD_CONV_W, GROUP_W), lambda b, g: (0, g)),
        pl.BlockSpec((SSD_CONV_W, D_STATE), lambda b, g: (0, n_xblk + g)),
        pl.BlockSpec((SSD_CONV_W, D_STATE), lambda b, g: (0, n_xblk + N_GROUPS + g)),
        pl.BlockSpec((1, GROUP_W), lambda b, g: (0, g)),
        pl.BlockSpec((1, D_STATE), lambda b, g: (0, n_xblk + g)),
        pl.BlockSpec((1, D_STATE), lambda b, g: (0, n_xblk + N_GROUPS + g)),
        pl.BlockSpec((1, DT_W), lambda b, g: (0, g)),
        pl.BlockSpec((1, DT_W), lambda b, g: (0, g)),
        pl.BlockSpec((GROUP_W, LANES), lambda b, g: (g, 0)),
    ]
    out_specs = [
        seq_blk(GROUP_W, lambda g: g),
        pl.BlockSpec((None, N_DIR, HEADS_PER_GROUP, HEAD_DIM, D_STATE), lambda b, g: (b, 0, g, 0, 0)),
    ]
    out_shape = [
        jax.ShapeDtypeStruct((batch, seq_len, D_SSD), F32),
        jax.ShapeDtypeStruct((batch, N_DIR, N_HEADS, HEAD_DIM, D_STATE), F32),
    ]
    padded = seq_len + 2 * CONV_PAD
    scratch = [
        pltpu.VMEM((padded, GROUP_W), F32),
        pltpu.VMEM((padded, D_STATE), F32),
        pltpu.VMEM((padded, D_STATE), F32),
        pltpu.VMEM((seq_len, D_STATE), F32),
        pltpu.VMEM((nc, 2 * HEADS_PER_GROUP, CHUNK), F32),
        pltpu.VMEM((nc, GROUP_W, CHUNK), F32),
        pltpu.VMEM((nc, GROUP_W, D_STATE), F32),
        pltpu.VMEM((nc, GROUP_W, D_STATE), F32),
        pltpu.VMEM((GROUP_W, D_STATE), F32),
        pltpu.VMEM((GROUP_W, D_STATE), F32),
    ]
    return pl.pallas_call(
        functools.partial(_ssd_kernel, seq_len=seq_len, zero_h0=zero_h0),
        grid=(batch, N_GROUPS),
        in_specs=in_specs,
        out_specs=out_specs,
        out_shape=out_shape,
        scratch_shapes=scratch,
        compiler_params=pltpu.CompilerParams(
            dimension_semantics=("arbitrary", "arbitrary"), vmem_limit_bytes=VMEM_LIMIT),
        name="ssd_branch",
    )(xs, bc, bc, dt, h0, p["conv_w"], p["conv_w"], p["conv_w"], p["conv_b"], p["conv_b"],
      p["conv_b"], p["a_log_k"], p["dt_bias_k"], p["d_skip_k"])


def _conformer_kernel(ga_ref, gb_ref, gs_ref, cw_ref, cb_ref, lng_ref, lnb_ref, o_ref, hp,
                      *, rows, lead):
    extra = ga_ref.shape[1:-1]
    ch = ga_ref.shape[-1]
    expand = (lambda v: v.reshape((1,) * (1 + len(extra)) + (ch,)))
    hp[0:lead] = jnp.zeros((lead,) + hp.shape[1:], F32)
    hp[lead + rows:2 * lead + rows] = jnp.zeros((lead,) + hp.shape[1:], F32)
    hp[lead:lead + rows] = ga_ref[...] * _sigmoid(gb_ref[...])
    acc = jnp.broadcast_to(expand(cb_ref[...]), ga_ref.shape)
    for k in range(CONF_CONV_W):
        lo = lead - CONF_HALF + k
        acc = acc + expand(cw_ref[k:k + 1, :]) * hp[lo:lo + rows]
    mu = jnp.mean(acc, axis=-1, keepdims=True)
    cen = acc - mu
    var = jnp.mean(cen * cen, axis=-1, keepdims=True)
    y = cen * lax.rsqrt(var + EPS) * expand(lng_ref[...]) + expand(lnb_ref[...])
    o_ref[...] = _silu(y) * _silu(gs_ref[...])


def _conformer_branch(ga, gb, gs, p, batch, rows, cols):
    vec = lambda *_: (0, 0)
    par_specs = [
        pl.BlockSpec((CONF_CONV_W, D_CONF), vec),
        pl.BlockSpec((1, D_CONF), vec),
        pl.BlockSpec((1, D_CONF), vec),
        pl.BlockSpec((1, D_CONF), vec),
    ]
    if cols == 1:
        lead = 2 * SUBLANES
        shape = (batch, rows, D_CONF)
        blk = pl.BlockSpec((None, rows, D_CONF), lambda b: (b, 0, 0))
        grid = (batch,)
        hp_shape = (rows + 2 * lead, D_CONF)
        sem = ("arbitrary",)
    else:
        lead = CONF_HALF
        shape = (batch, rows, cols, D_CONF)
        blk = pl.BlockSpec((None, rows, SUBLANES, D_CONF), lambda b, w: (b, 0, w, 0))
        grid = (batch, cols // SUBLANES)
        hp_shape = (rows + 2 * lead, SUBLANES, D_CONF)
        sem = ("arbitrary", "arbitrary")
    out = pl.pallas_call(
        functools.partial(_conformer_kernel, rows=rows, lead=lead),
        grid=grid,
        in_specs=[blk, blk, blk] + par_specs,
        out_specs=blk,
        out_shape=jax.ShapeDtypeStruct(shape, F32),
        scratch_shapes=[pltpu.VMEM(hp_shape, F32)],
        compiler_params=pltpu.CompilerParams(dimension_semantics=sem, vmem_limit_bytes=VMEM_LIMIT),
        name="conformer_branch",
    )(ga.reshape(shape), gb.reshape(shape), gs.reshape(shape),
      p["conf_w"], p["conf_b"], p["ln_g"], p["ln_b"])
    return out.reshape(batch * rows * cols, D_CONF)


def _outproj_kernel(x_ref, y_ref, z_ref, yc_ref, gate_ref, ng_ref, gp_ref, w_ref, o_ref):
    y = y_ref[...] * _silu(z_ref[...])
    ms = jnp.mean(y * y, axis=-1, keepdims=True)
    y = y * lax.rsqrt(ms + EPS) * ng_ref[...]
    out = jnp.dot(y.astype(BF16), w_ref[0:D_SSD, :], preferred_element_type=F32)
    out = out + jnp.dot(yc_ref[...].astype(BF16), w_ref[D_SSD:D_SSD + D_CONF, :],
                        preferred_element_type=F32)
    ms2 = jnp.mean(out * out, axis=-1, keepdims=True)
    o_ref[...] = x_ref[...] + gate_ref[...] * (out * lax.rsqrt(ms2 + EPS) * gp_ref[...])


def _out_projection(x2, y_ssd, z, y_conf, gate, norm_g, g_post, w_out_k, rows_per_mod):
    t = x2.shape[0]
    tiles_per_mod = rows_per_mod // ROW_TILE
    row = pl.BlockSpec((ROW_TILE, D_MODEL), lambda i: (i, 0))
    vec = pl.BlockSpec((1, D_MODEL), lambda i: (0, 0))
    return pl.pallas_call(
        _outproj_kernel,
        grid=(t // ROW_TILE,),
        in_specs=[row, row, row, row,
                  pl.BlockSpec((None, 1, D_MODEL), lambda i: (i // tiles_per_mod, 0, 0)),
                  vec, vec,
                  pl.BlockSpec((D_SSD + D_CONF, D_MODEL), lambda i: (0, 0))],
        out_specs=row,
        out_shape=jax.ShapeDtypeStruct((t, D_MODEL), F32),
        compiler_params=pltpu.CompilerParams(
            dimension_semantics=("arbitrary",), vmem_limit_bytes=VMEM_LIMIT),
        name="out_projection",
    )(x2, y_ssd, z, y_conf, gate, norm_g, g_post, w_out_k)


def _layer_params(l, g_pre, g_post, w_in, ssd_conv_w, ssd_conv_b, ssd_a_log, ssd_dt_bias, ssd_d,
                  ssd_norm_g, conf_conv_w, conf_conv_b, conf_ln_g, conf_ln_b, w_out):
    w = w_in[l]

    def per_group(a):
        a = a.reshape(N_DIR, N_GROUPS, HEADS_PER_GROUP).transpose(1, 0, 2)
        a = a.reshape(N_GROUPS, N_DIR * HEADS_PER_GROUP)
        a = jnp.pad(a, ((0, 0), (0, DT_W - N_DIR * HEADS_PER_GROUP)))
        return a.reshape(1, N_GROUPS * DT_W)

    w_dt = w[:, I_XBC:I_DT].reshape(D_MODEL, N_DIR, N_GROUPS, HEADS_PER_GROUP)
    w_dt = w_dt.transpose(0, 2, 1, 3).reshape(D_MODEL, N_GROUPS, N_DIR * HEADS_PER_GROUP)
    w_dt = jnp.pad(w_dt, ((0, 0), (0, 0), (0, DT_W - N_DIR * HEADS_PER_GROUP)))
    w_in_k = jnp.concatenate(
        [w[:, 0:I_XBC], w[:, I_DT:IN_COLS], w_dt.reshape(D_MODEL, N_GROUPS * DT_W)],
        axis=1).astype(BF16)
    d_rows = jnp.broadcast_to(jnp.repeat(ssd_d[l], HEAD_DIM)[:, None], (D_SSD, LANES))
    return dict(
        g_pre=g_pre[l].reshape(1, D_MODEL), g_post=g_post[l].reshape(1, D_MODEL),
        w_in_k=w_in_k, w_out_k=w_out[l].astype(BF16),
        conv_w=ssd_conv_w[l], conv_b=ssd_conv_b[l].reshape(1, -1),
        a_log_k=per_group(ssd_a_log[l]), dt_bias_k=per_group(ssd_dt_bias[l]), d_skip_k=d_rows,
        norm_g=ssd_norm_g[l].reshape(1, D_SSD),
        conf_w=conf_conv_w[l], conf_b=conf_conv_b[l].reshape(1, D_CONF),
        ln_g=conf_ln_g[l].reshape(1, D_CONF), ln_b=conf_ln_b[l].reshape(1, D_CONF),
    )


def _trunk_layer(x, mod, grid, h0, h0_layer, p):
    batch, seq_len, _ = x.shape
    rows, cols = grid
    t = batch * seq_len
    rows_per_mod = t // mod.shape[0]
    shift, scale, gate = (mod[:, i * D_MODEL:(i + 1) * D_MODEL].reshape(-1, 1, D_MODEL)
                          for i in range(3))
    x2 = x.reshape(t, D_MODEL)
    z, xs, bc, ga, gb, gs, dt = _in_projection(x2, shift, scale, p["g_pre"], p["w_in_k"],
                                               rows_per_mod)
    y_ssd, h_fin = _ssd_branch(xs.reshape(batch, seq_len, -1), bc.reshape(batch, seq_len, -1),
                               dt.reshape(batch, seq_len, -1), h0, h0_layer, p, batch, seq_len)
    y_conf = _conformer_branch(ga, gb, gs, p, batch, rows, cols)
    out = _out_projection(x2, y_ssd.reshape(t, D_SSD), z, y_conf, gate, p["norm_g"], p["g_post"],
                          p["w_out_k"], rows_per_mod)
    return out.reshape(batch, seq_len, D_MODEL), h_fin


def kernel(x_prompt, x_sample, state_ssd, c, c_ctx, w_mod, b_mod, g_pre, g_post, w_in, ssd_conv_w, ssd_conv_b, ssd_a_log, ssd_dt_bias, ssd_d, ssd_norm_g, conf_conv_w, conf_conv_b, conf_ln_g, conf_ln_b, w_out):
    dec_batch = x_sample.shape[0]
    ctx_len = x_prompt.shape[1]
    lat_rows = x_sample.shape[1] // GRID_W
    cvec = jnp.zeros((SUBLANES, D_MODEL), F32).at[0].set(c_ctx).at[1:1 + dec_batch].set(c)
    mods = _modulation(cvec, w_mod, b_mod)
    xp, xs = x_prompt, x_sample
    ctx_states = []
    for l in range(DEPTH):
        p = _layer_params(l, g_pre, g_post, w_in, ssd_conv_w, ssd_conv_b, ssd_a_log, ssd_dt_bias,
                          ssd_d, ssd_norm_g, conf_conv_w, conf_conv_b, conf_ln_g, conf_ln_b, w_out)
        xp, st = _trunk_layer(xp, mods[l, 0:1], (ctx_len, 1), None, l, p)
        ctx_states.append(st)
        xs, _ = _trunk_layer(xs, mods[l, 1:1 + dec_batch], (lat_rows, GRID_W), state_ssd, l, p)
    return (xp, xs, jnp.stack(ctx_states, axis=1))
```

```python
import functools
import math

import jax
import jax.numpy as jnp
from jax import lax
from jax.experimental import pallas as pl
from jax.experimental.pallas import tpu as pltpu

F32 = jnp.float32
BF16 = jnp.bfloat16

D_MODEL = 1024
DEPTH = 2
GRID_W = 64
D_SSD = 1024
D_CONF = 1024
HEAD_DIM = 64
N_HEADS = 16
N_GROUPS = 2
HEADS_PER_GROUP = N_HEADS // N_GROUPS
GROUP_W = HEADS_PER_GROUP * HEAD_DIM
D_STATE = 128
SSD_CONV_W = 5
CHUNK = 128
HALF = CHUNK // 2
N_DIR = 2
CONF_CONV_W = 31
CONF_HALF = CONF_CONV_W // 2
EPS = 1e-6
LOG2E = math.log2(math.e)

I_Z = D_SSD
I_X = I_Z + D_SSD
I_XBC = I_X + 2 * N_GROUPS * D_STATE
I_DT = I_XBC + N_DIR * N_HEADS
I_GA = I_DT + D_CONF
I_GB = I_GA + D_CONF
IN_COLS = I_GB + D_CONF

LANES = 128
SUBLANES = 8
DT_W = LANES
X_SLABS = D_SSD // LANES
GROUP_SLABS = GROUP_W // LANES
BC_SLABS = 2 * N_GROUPS * D_STATE // LANES
CONF_SLABS = D_CONF // LANES

VMEM_LIMIT = 56 * 1024 * 1024
ROW_TILE = 256
CONV_PAD = 8
NT_DIMS = (((1,), (1,)), ((), ()))


def _sigmoid(x):
    return 0.5 * jnp.tanh(0.5 * x) + 0.5


def _silu(x):
    return x * _sigmoid(x)


def _softplus(x):
    return jnp.maximum(x, 0.0) + jnp.log1p(jnp.exp(-jnp.abs(x)))


def _split3(x):
    hi = x.astype(BF16)
    r1 = x - hi.astype(F32)
    mid = r1.astype(BF16)
    lo = (r1 - mid.astype(F32)).astype(BF16)
    return hi, mid, lo


def _lane_slab(t):
    return slice(t * LANES, (t + 1) * LANES)


def _mod_kernel(c_ref, w_ref, b_ref, o_ref):
    s = _silu(c_ref[...])
    o_ref[...] = jnp.dot(s, w_ref[...], preferred_element_type=F32,
                         precision=lax.Precision.HIGHEST) + b_ref[...]


def _modulation(cvec, w_mod, b_mod):
    return pl.pallas_call(
        _mod_kernel,
        grid=(DEPTH, 3),
        in_specs=[
            pl.BlockSpec((SUBLANES, D_MODEL), lambda l, j: (0, 0)),
            pl.BlockSpec((None, D_MODEL, D_MODEL), lambda l, j: (l, 0, j)),
            pl.BlockSpec((None, 1, D_MODEL), lambda l, j: (l, 0, j)),
        ],
        out_specs=pl.BlockSpec((None, SUBLANES, D_MODEL), lambda l, j: (l, 0, j)),
        out_shape=jax.ShapeDtypeStruct((DEPTH, SUBLANES, 3 * D_MODEL), F32),
        compiler_params=pltpu.CompilerParams(
            dimension_semantics=("arbitrary", "arbitrary"), vmem_limit_bytes=VMEM_LIMIT),
        name="modulation",
    )(cvec, w_mod, b_mod.reshape(DEPTH, 1, 3 * D_MODEL))


def _inproj_kernel(x_ref, shift_ref, scale_ref, g_ref, ws_ref, wc_ref, wd_ref,
                   z_ref, xs_ref, bc_ref, ga_ref, gb_ref, gs_ref, dt_ref):
    x = x_ref[...]
    ms = jnp.mean(x * x, axis=-1, keepdims=True)
    h = (x * lax.rsqrt(ms + EPS) * g_ref[...]) * (1.0 + scale_ref[...]) + shift_ref[...]
    hb = h.astype(BF16)

    def proj(w_ref, lo, hi):
        return jnp.dot(hb, w_ref[:, lo:hi], preferred_element_type=F32)

    z_ref[...] = proj(ws_ref, 0, I_Z)
    xs = proj(ws_ref, I_Z, I_X)
    for t in range(X_SLABS):
        xs_ref[t] = xs[:, _lane_slab(t)]
    bc = proj(ws_ref, I_X, I_XBC)
    for t in range(BC_SLABS):
        bc_ref[t] = bc[:, _lane_slab(t)]
    ga_ref[...] = proj(wc_ref, 0, D_CONF)
    gb_ref[...] = proj(wc_ref, D_CONF, 2 * D_CONF)
    gs_ref[...] = proj(wc_ref, 2 * D_CONF, 3 * D_CONF)
    dt_ref[...] = jnp.dot(hb, wd_ref[...], preferred_element_type=F32)


def _in_projection(x2, shift, scale, p, rows_per_mod):
    t = x2.shape[0]
    tiles_per_mod = rows_per_mod // ROW_TILE
    row = lambda i: (i, 0)
    slab = lambda i: (0, i, 0)
    mod = lambda i: (i // tiles_per_mod, 0, 0)
    const = lambda i: (0, 0)
    wide = lambda w: (pl.BlockSpec((ROW_TILE, w), row), jax.ShapeDtypeStruct((t, w), F32))
    slabs = lambda n: (pl.BlockSpec((n, ROW_TILE, LANES), slab),
                       jax.ShapeDtypeStruct((n, t, LANES), F32))
    outs = [wide(D_SSD), slabs(X_SLABS), slabs(BC_SLABS), wide(D_CONF), wide(D_CONF),
            wide(D_CONF), wide(N_GROUPS * DT_W)]
    return pl.pallas_call(
        _inproj_kernel,
        grid=(t // ROW_TILE,),
        in_specs=[
            pl.BlockSpec((ROW_TILE, D_MODEL), row),
            pl.BlockSpec((None, 1, D_MODEL), mod),
            pl.BlockSpec((None, 1, D_MODEL), mod),
            pl.BlockSpec((1, D_MODEL), const),
            pl.BlockSpec((D_MODEL, I_XBC), const),
            pl.BlockSpec((D_MODEL, 3 * D_CONF), const),
            pl.BlockSpec((D_MODEL, N_GROUPS * DT_W), const),
        ],
        out_specs=[o[0] for o in outs],
        out_shape=[o[1] for o in outs],
        compiler_params=pltpu.CompilerParams(
            dimension_semantics=("arbitrary",), vmem_limit_bytes=VMEM_LIMIT),
        name="in_projection",
    )(x2, shift, scale, p["g_pre"], p["w_ssd"], p["w_conf"], p["w_dt"])


def _ssd_kernel(*refs, seq_len, zero_h0, alias_state):
    (xs_ref, b_ref, c_ref, dt_ref, h0_ref, cwx_ref, cwb_ref, cwc_ref,
     cbx_ref, cbb_ref, cbc_ref, alog_ref, dtb_ref, dsk_ref) = refs[:14]
    refs = refs[14 + (1 if alias_state else 0):]
    y_ref, hfin_ref, xpad, bpad, cpad, ccv, cums, yt, stf, stb, s_f, s_b = refs
    nc = seq_len // CHUNK
    hw = HEAD_DIM

    zeros = jnp.zeros((CONV_PAD, LANES), F32)
    pads = [(xpad.at[t], xs_ref.at[t]) for t in range(GROUP_SLABS)] + [(bpad, b_ref), (cpad, c_ref)]
    for pad, src in pads:
        pad[0:CONV_PAD, :] = zeros
        pad[CONV_PAD + seq_len:2 * CONV_PAD + seq_len, :] = zeros
        pad[CONV_PAD:CONV_PAD + seq_len, :] = src[...]

    def token_pos(idx):
        return jnp.where(idx < HALF, 2 * idx, 2 * (idx - HALF) + 1)

    row_p = token_pos(lax.broadcasted_iota(jnp.int32, (CHUNK, CHUNK), 0))
    col_p = token_pos(lax.broadcasted_iota(jnp.int32, (CHUNK, CHUNK), 1))
    lower = row_p >= col_p
    upper = row_p <= col_p
    tri = jnp.concatenate([jnp.where(lower, 1.0, 0.0), jnp.where(upper, 1.0, 0.0)],
                          axis=1).astype(BF16)
    lane = lax.broadcasted_iota(jnp.int32, (CHUNK, LANES), 1)
    fwd_lanes = lane < HEADS_PER_GROUP
    bwd_lanes = (lane >= HEADS_PER_GROUP) & (lane < 2 * HEADS_PER_GROUP)
    a_neg2 = -jnp.exp(alog_ref[...]) * LOG2E
    dt_bias = dtb_ref[...]
    last, first = CHUNK - 1, 0

    def conv_silu(pad, w_ref, bias_ref, lanes, o):
        taps = {s: pad[pl.ds(o + CONV_PAD + s, HALF, stride=2), :]
                for s in range(-(SSD_CONV_W // 2), SSD_CONV_W // 2 + 2)}
        halves = []
        for e in range(2):
            acc = jnp.broadcast_to(bias_ref[:, lanes], (HALF, LANES))
            for k in range(SSD_CONV_W):
                acc = acc + w_ref[k:k + 1, lanes] * taps[e + k - SSD_CONV_W // 2]
            halves.append(acc)
        return _silu(jnp.concatenate(halves, axis=0))

    def chunk_local(c, carry):
        o = pl.multiple_of(c * CHUNK, CHUNK)
        xs_c = jnp.concatenate(
            [conv_silu(xpad.at[t], cwx_ref, cbx_ref, _lane_slab(t), o) for t in range(GROUP_SLABS)],
            axis=1)
        b_c = conv_silu(bpad, cwb_ref, cbb_ref, slice(None), o)
        c_c = conv_silu(cpad, cwc_ref, cbc_ref, slice(None), o)
        ccv[pl.ds(o, CHUNK), :] = c_c
        b_bf = b_c.astype(BF16)
        cb = lax.dot_general(c_c.astype(BF16), b_bf, NT_DIMS, preferred_element_type=F32)
        xs_t = xs_c.T

        dt_raw = jnp.concatenate([dt_ref[pl.ds(o + e, HALF, stride=2), :] for e in range(2)], axis=0)
        dt_c = _softplus(dt_raw + dt_bias)
        la = dt_c * a_neg2
        stacked = jnp.concatenate(
            [jnp.where(fwd_lanes, la, 0.0), jnp.where(bwd_lanes, la, 0.0)], axis=0)
        cum = None
        for part in _split3(stacked):
            term = jnp.dot(tri, part, preferred_element_type=F32)
            cum = term if cum is None else cum + term
        cum_t = cum.T
        dt_t = dt_c.T
        cums[c] = cum_t[0:2 * HEADS_PER_GROUP, :]

        y_rows = []
        for j in range(HEADS_PER_GROUP):
            x_h = xs_t[j * hw:(j + 1) * hw, :]
            y_rows.append(dsk_ref[j * hw:(j + 1) * hw, :] * x_h)
        for d, (mask, st_ref) in enumerate(((lower, stf), (upper, stb))):
            xgd_rows = []
            for j in range(HEADS_PER_GROUP):
                q = d * HEADS_PER_GROUP + j
                crow = cum_t[q:q + 1, :]
                diff = cum[:, q:q + 1] - crow
                decay = jnp.exp2(jnp.where(mask, diff, -jnp.inf))
                m_h = (cb * decay).astype(BF16)
                xg_t = xs_t[j * hw:(j + 1) * hw, :] * dt_t[q:q + 1, :]
                y_rows[j] = y_rows[j] + lax.dot_general(
                    xg_t.astype(BF16), m_h, NT_DIMS, preferred_element_type=F32)
                tot = crow[:, last:last + 1] if d == 0 else crow[:, first:first + 1]
                xgd_rows.append(xg_t * jnp.exp2(tot - crow))
            xgd = jnp.concatenate(xgd_rows, axis=0).astype(BF16)
            st_ref[c] = jnp.dot(xgd, b_bf, preferred_element_type=F32)
        yt[c] = jnp.concatenate(y_rows, axis=0)
        return carry

    lax.fori_loop(0, nc, chunk_local, 0, unroll=2)

    for d, s_ref in enumerate((s_f, s_b)):
        if zero_h0:
            s_ref[...] = jnp.zeros(s_ref.shape, F32)
        else:
            s_ref[...] = h0_ref[d].reshape(GROUP_W, D_STATE)

    def head_rows(ct, base, col):
        parts = []
        for j in range(HEADS_PER_GROUP):
            r = ct[base + j:base + j + 1, :]
            if col is not None:
                r = r[:, col:col + 1]
            parts.append(jnp.broadcast_to(jnp.exp2(r), (hw, CHUNK)))
        return jnp.concatenate(parts, axis=0)

    def recur(i, carry):
        for d, (s_ref, st_ref) in enumerate(((s_f, stf), (s_b, stb))):
            c = i if d == 0 else nc - 1 - i
            o = pl.multiple_of(c * CHUNK, CHUNK)
            s = s_ref[...]
            c_bf = ccv[pl.ds(o, CHUNK), :].astype(BF16)
            y_off = lax.dot_general(s.astype(BF16), c_bf, NT_DIMS, preferred_element_type=F32)
            ct = cums[c]
            base = d * HEADS_PER_GROUP
            yt[c] = yt[c] + y_off * head_rows(ct, base, None)
            s_ref[...] = s * head_rows(ct, base, last if d == 0 else first) + st_ref[c]
        return carry

    lax.fori_loop(0, nc, recur, 0)

    hfin_ref[0] = s_f[...].reshape(HEADS_PER_GROUP, HEAD_DIM, D_STATE)
    hfin_ref[1] = s_b[...].reshape(HEADS_PER_GROUP, HEAD_DIM, D_STATE)

    def write_back(c, carry):
        o = pl.multiple_of(c * CHUNK, CHUNK)
        y_c = yt[c].T
        for t in range(GROUP_SLABS):
            y_ref[t, pl.ds(o, HALF, stride=2), :] = y_c[0:HALF, _lane_slab(t)]
            y_ref[t, pl.ds(o + 1, HALF, stride=2), :] = y_c[HALF:CHUNK, _lane_slab(t)]
        return carry

    lax.fori_loop(0, nc, write_back, 0)


def _ssd_branch(xs, bc, dt, h0, layer, states, p, batch, seq_len):
    nc = seq_len // CHUNK
    zero_h0 = h0 is None
    alias_state = states is not None
    state_blk = (None, None, N_DIR, HEADS_PER_GROUP, HEAD_DIM, D_STATE)
    if zero_h0:
        h0 = jnp.zeros((1, 1, N_DIR, HEADS_PER_GROUP, HEAD_DIM, D_STATE), F32)
        h0_spec = pl.BlockSpec(state_blk, lambda b, g: (0, 0, 0, 0, 0, 0))
    else:
        h0_spec = pl.BlockSpec(state_blk, lambda b, g: (b, layer, 0, g, 0, 0))
    n_xblk = D_SSD // D_STATE
    in_specs = [
        pl.BlockSpec((GROUP_SLABS, seq_len, LANES), lambda b, g: (g, b, 0)),
        pl.BlockSpec((None, seq_len, LANES), lambda b, g: (g, b, 0)),
        pl.BlockSpec((None, seq_len, LANES), lambda b, g: (N_GROUPS + g, b, 0)),
        pl.BlockSpec((seq_len, DT_W), lambda b, g: (b, g)),
        h0_spec,
        pl.BlockSpec((SSD_CONV_W, GROUP_W), lambda b, g: (0, g)),
        pl.BlockSpec((SSD_CONV_W, D_STATE), lambda b, g: (0, n_xblk + g)),
        pl.BlockSpec((SSD_CONV_W, D_STATE), lambda b, g: (0, n_xblk + N_GROUPS + g)),
        pl.BlockSpec((1, GROUP_W), lambda b, g: (0, g)),
        pl.BlockSpec((1, D_STATE), lambda b, g: (0, n_xblk + g)),
        pl.BlockSpec((1, D_STATE), lambda b, g: (0, n_xblk + N_GROUPS + g)),
        pl.BlockSpec((1, DT_W), lambda b, g: (0, g)),
        pl.BlockSpec((1, DT_W), lambda b, g: (0, g)),
        pl.BlockSpec((GROUP_W, LANES), lambda b, g: (g, 0)),
    ]
    args = [xs, bc, bc, dt, h0, p["conv_w"], p["conv_w"], p["conv_w"], p["conv_b"], p["conv_b"],
            p["conv_b"], p["a_log_k"], p["dt_bias_k"], p["d_skip_k"]]
    aliases = {}
    if alias_state:
        in_specs.append(pl.BlockSpec(memory_space=pl.ANY))
        args.append(states)
        aliases = {len(args) - 1: 1}
    out_specs = [
        pl.BlockSpec((GROUP_SLABS, seq_len, LANES), lambda b, g: (g, b, 0)),
        pl.BlockSpec(state_blk, lambda b, g: (b, layer, 0, g, 0, 0)),
    ]
    out_shape = [
        jax.ShapeDtypeStruct((X_SLABS, batch * seq_len, LANES), F32),
        jax.ShapeDtypeStruct((batch, DEPTH, N_DIR, N_HEADS, HEAD_DIM, D_STATE), F32),
    ]
    padded = seq_len + 2 * CONV_PAD
    scratch = [
        pltpu.VMEM((GROUP_SLABS, padded, LANES), F32),
        pltpu.VMEM((padded, D_STATE), F32),
        pltpu.VMEM((padded, D_STATE), F32),
        pltpu.VMEM((seq_len, D_STATE), F32),
        pltpu.VMEM((nc, 2 * HEADS_PER_GROUP, CHUNK), F32),
        pltpu.VMEM((nc, GROUP_W, CHUNK), F32),
        pltpu.VMEM((nc, GROUP_W, D_STATE), F32),
        pltpu.VMEM((nc, GROUP_W, D_STATE), F32),
        pltpu.VMEM((GROUP_W, D_STATE), F32),
        pltpu.VMEM((GROUP_W, D_STATE), F32),
    ]
    return pl.pallas_call(
        functools.partial(_ssd_kernel, seq_len=seq_len, zero_h0=zero_h0, alias_state=alias_state),
        grid=(batch, N_GROUPS),
        in_specs=in_specs,
        out_specs=out_specs,
        out_shape=out_shape,
        scratch_shapes=scratch,
        input_output_aliases=aliases,
        compiler_params=pltpu.CompilerParams(
            dimension_semantics=("arbitrary", "arbitrary"), vmem_limit_bytes=VMEM_LIMIT),
        name="ssd_branch",
    )(*args)


def _conformer_kernel(ga_ref, gb_ref, gs_ref, cw_ref, cb_ref, lng_ref, lnb_ref, o_ref,
                      hp, wrow, cv, *, rows, lead, token_major):
    inner = ga_ref.shape[1:-1]
    vec = (lambda v: v.reshape((1,) * (1 + len(inner)) + (v.shape[-1],)))
    bias_row = CONF_CONV_W
    zeros = jnp.zeros((lead,) + hp.shape[2:], F32)
    for t in range(CONF_SLABS):
        lanes = _lane_slab(t)
        wrow[t, 0:CONF_CONV_W, :] = cw_ref[:, lanes]
        wrow[t, bias_row:bias_row + 1, :] = cb_ref[:, lanes]
        hp[t, 0:lead] = zeros
        hp[t, lead + rows:2 * lead + rows] = zeros
        hp[t, lead:lead + rows] = ga_ref[..., lanes] * _sigmoid(gb_ref[..., lanes])

    def w_at(t, k):
        return vec(wrow[t, k:k + 1, :])

    def slab_conv(t, carry):
        if token_major:
            half = rows // 2
            acc = [jnp.broadcast_to(w_at(t, bias_row), (half, LANES)) for _ in range(2)]
            for s in range(CONF_CONV_W + 1):
                v = hp[t, pl.ds(lead - CONF_HALF + s, half, stride=2), :]
                for e in range(2):
                    if 0 <= s - e < CONF_CONV_W:
                        acc[e] = acc[e] + w_at(t, s - e) * v
            for e in range(2):
                cv[t, pl.ds(e, half, stride=2), :] = acc[e]
        else:
            acc = [jnp.broadcast_to(w_at(t, bias_row), (1,) + hp.shape[2:]) for _ in range(rows)]
            for i in range(rows + 2 * lead):
                v = hp[t, i:i + 1]
                for r in range(max(0, i - 2 * CONF_HALF), min(rows - 1, i) + 1):
                    acc[r] = acc[r] + w_at(t, i - r) * v
            for r in range(rows):
                cv[t, r:r + 1] = acc[r]
        return carry

    lax.fori_loop(0, CONF_SLABS, slab_conv, 0)

    acc = jnp.concatenate([cv[t] for t in range(CONF_SLABS)], axis=-1)
    mu = jnp.mean(acc, axis=-1, keepdims=True)
    cen = acc - mu
    var = jnp.mean(cen * cen, axis=-1, keepdims=True)
    y = cen * lax.rsqrt(var + EPS) * vec(lng_ref[...]) + vec(lnb_ref[...])
    o_ref[...] = _silu(y) * _silu(gs_ref[...])


def _conformer_branch(ga, gb, gs, p, batch, rows, cols):
    vec = lambda *_: (0, 0)
    par_specs = [
        pl.BlockSpec((CONF_CONV_W, D_CONF), vec),
        pl.BlockSpec((1, D_CONF), vec),
        pl.BlockSpec((1, D_CONF), vec),
        pl.BlockSpec((1, D_CONF), vec),
    ]
    token_major = cols == 1
    if token_major:
        lead = 2 * SUBLANES
        shape = (batch, rows, D_CONF)
        blk = pl.BlockSpec((None, rows, D_CONF), lambda b: (b, 0, 0))
        grid = (batch,)
        inner = ()
        sem = ("arbitrary",)
    else:
        lead = CONF_HALF
        shape = (batch, rows, cols, D_CONF)
        blk = pl.BlockSpec((None, rows, SUBLANES, D_CONF), lambda b, w: (b, 0, w, 0))
        grid = (batch, cols // SUBLANES)
        inner = (SUBLANES,)
        sem = ("arbitrary", "arbitrary")
    scratch = [
        pltpu.VMEM((CONF_SLABS, rows + 2 * lead) + inner + (LANES,), F32),
        pltpu.VMEM((CONF_SLABS, 4 * SUBLANES, LANES), F32),
        pltpu.VMEM((CONF_SLABS, rows) + inner + (LANES,), F32),
    ]
    out = pl.pallas_call(
        functools.partial(_conformer_kernel, rows=rows, lead=lead, token_major=token_major),
        grid=grid,
        in_specs=[blk, blk, blk] + par_specs,
        out_specs=blk,
        out_shape=jax.ShapeDtypeStruct(shape, F32),
        scratch_shapes=scratch,
        compiler_params=pltpu.CompilerParams(dimension_semantics=sem, vmem_limit_bytes=VMEM_LIMIT),
        name="conformer_branch",
    )(ga.reshape(shape), gb.reshape(shape), gs.reshape(shape),
      p["conf_w"], p["conf_b"], p["ln_g"], p["ln_b"])
    return out.reshape(batch * rows * cols, D_CONF)


def _outproj_kernel(x_ref, y_ref, z_ref, yc_ref, gate_ref, ng_ref, gp_ref, w_ref, o_ref):
    y = jnp.concatenate([y_ref[t] for t in range(X_SLABS)], axis=1) * _silu(z_ref[...])
    ms = jnp.mean(y * y, axis=-1, keepdims=True)
    y = y * lax.rsqrt(ms + EPS) * ng_ref[...]
    out = jnp.dot(y.astype(BF16), w_ref[0:D_SSD, :], preferred_element_type=F32)
    out = out + jnp.dot(yc_ref[...].astype(BF16), w_ref[D_SSD:D_SSD + D_CONF, :],
                        preferred_element_type=F32)
    ms2 = jnp.mean(out * out, axis=-1, keepdims=True)
    o_ref[...] = x_ref[...] + gate_ref[...] * (out * lax.rsqrt(ms2 + EPS) * gp_ref[...])


def _out_projection(x2, y_ssd, z, y_conf, gate, p, rows_per_mod):
    t = x2.shape[0]
    tiles_per_mod = rows_per_mod // ROW_TILE
    row = pl.BlockSpec((ROW_TILE, D_MODEL), lambda i: (i, 0))
    vec = pl.BlockSpec((1, D_MODEL), lambda i: (0, 0))
    return pl.pallas_call(
        _outproj_kernel,
        grid=(t // ROW_TILE,),
        in_specs=[row,
                  pl.BlockSpec((X_SLABS, ROW_TILE, LANES), lambda i: (0, i, 0)),
                  row, row,
                  pl.BlockSpec((None, 1, D_MODEL), lambda i: (i // tiles_per_mod, 0, 0)),
                  vec, vec,
                  pl.BlockSpec((D_SSD + D_CONF, D_MODEL), lambda i: (0, 0))],
        out_specs=row,
        out_shape=jax.ShapeDtypeStruct((t, D_MODEL), F32),
        compiler_params=pltpu.CompilerParams(
            dimension_semantics=("arbitrary",), vmem_limit_bytes=VMEM_LIMIT),
        name="out_projection",
    )(x2, y_ssd, z, y_conf, gate, p["norm_g"], p["g_post"], p["w_out_k"])


def _layer_params(l, g_pre, g_post, w_in, ssd_conv_w, ssd_conv_b, ssd_a_log, ssd_dt_bias, ssd_d,
                  ssd_norm_g, conf_conv_w, conf_conv_b, conf_ln_g, conf_ln_b, w_out):
    w = w_in[l]

    def per_group(a):
        a = a.reshape(N_DIR, N_GROUPS, HEADS_PER_GROUP).transpose(1, 0, 2)
        a = a.reshape(N_GROUPS, N_DIR * HEADS_PER_GROUP)
        a = jnp.pad(a, ((0, 0), (0, DT_W - N_DIR * HEADS_PER_GROUP)))
        return a.reshape(1, N_GROUPS * DT_W)

    w_dt = w[:, I_XBC:I_DT].reshape(D_MODEL, N_DIR, N_GROUPS, HEADS_PER_GROUP)
    w_dt = w_dt.transpose(0, 2, 1, 3).reshape(D_MODEL, N_GROUPS, N_DIR * HEADS_PER_GROUP)
    w_dt = jnp.pad(w_dt, ((0, 0), (0, 0), (0, DT_W - N_DIR * HEADS_PER_GROUP)))
    d_rows = jnp.broadcast_to(jnp.repeat(ssd_d[l], HEAD_DIM)[:, None], (D_SSD, LANES))
    return dict(
        g_pre=g_pre[l].reshape(1, D_MODEL), g_post=g_post[l].reshape(1, D_MODEL),
        w_ssd=w[:, 0:I_XBC].astype(BF16), w_conf=w[:, I_DT:IN_COLS].astype(BF16),
        w_dt=w_dt.reshape(D_MODEL, N_GROUPS * DT_W).astype(BF16),
        w_out_k=w_out[l].astype(BF16),
        conv_w=ssd_conv_w[l], conv_b=ssd_conv_b[l].reshape(1, -1),
        a_log_k=per_group(ssd_a_log[l]), dt_bias_k=per_group(ssd_dt_bias[l]), d_skip_k=d_rows,
        norm_g=ssd_norm_g[l].reshape(1, D_SSD),
        conf_w=conf_conv_w[l], conf_b=conf_conv_b[l].reshape(1, D_CONF),
        ln_g=conf_ln_g[l].reshape(1, D_CONF), ln_b=conf_ln_b[l].reshape(1, D_CONF),
    )


def _trunk_layer(x, mod, grid, h0, layer, states, p):
    batch, seq_len, _ = x.shape
    rows, cols = grid
    t = batch * seq_len
    rows_per_mod = t // mod.shape[0]
    shift, scale, gate = (mod[:, i * D_MODEL:(i + 1) * D_MODEL].reshape(-1, 1, D_MODEL)
                          for i in range(3))
    x2 = x.reshape(t, D_MODEL)
    z, xs, bc, ga, gb, gs, dt = _in_projection(x2, shift, scale, p, rows_per_mod)
    y_ssd, states = _ssd_branch(xs, bc, dt, h0, layer, states, p, batch, seq_len)
    y_conf = _conformer_branch(ga, gb, gs, p, batch, rows, cols)
    out = _out_projection(x2, y_ssd, z, y_conf, gate, p, rows_per_mod)
    return out.reshape(batch, seq_len, D_MODEL), states


def kernel(x_prompt, x_sample, state_ssd, c, c_ctx, w_mod, b_mod, g_pre, g_post, w_in, ssd_conv_w, ssd_conv_b, ssd_a_log, ssd_dt_bias, ssd_d, ssd_norm_g, conf_conv_w, conf_conv_b, conf_ln_g, conf_ln_b, w_out):
    dec_batch = x_sample.shape[0]
    ctx_len = x_prompt.shape[1]
    lat_rows = x_sample.shape[1] // GRID_W
    cvec = jnp.zeros((SUBLANES, D_MODEL), F32).at[0].set(c_ctx).at[1:1 + dec_batch].set(c)
    mods = _modulation(cvec, w_mod, b_mod)
    xp, xs = x_prompt, x_sample
    ctx_states = None
    for l in range(DEPTH):
        p = _layer_params(l, g_pre, g_post, w_in, ssd_conv_w, ssd_conv_b, ssd_a_log, ssd_dt_bias,
                          ssd_d, ssd_norm_g, conf_conv_w, conf_conv_b, conf_ln_g, conf_ln_b, w_out)
        xp, ctx_states = _trunk_layer(xp, mods[l, 0:1], (ctx_len, 1), None, l, ctx_states, p)
        xs, _ = _trunk_layer(xs, mods[l, 1:1 + dec_batch], (lat_rows, GRID_W), state_ssd, l, None, p)
    return (xp, xs, ctx_states)
```
